```python
import jax, jax.numpy as jnp
from jax import lax
import numpy as np

D_MODEL = 2048
BATCH = 32
SEQ = 256
DEPTH = 4
DEC_BATCH = 2
DEC_SEQ = 1024
PAST_LEN = 512

GRID_W = 64
HEAD_DIM = 128
MIX_WIDTH = D_MODEL
D_FF = 5504
N_MOD = 9
EPS = 1e-6
NEG = -1e30
ROPE_BASE = 10000.0
CHUNK = 128
A_WIDTH = MIX_WIDTH // 2
A_GROUP_DIM = 128
A_GROUPS = A_WIDTH // A_GROUP_DIM
B_WIDTH = MIX_WIDTH // 2
B_GROUPS = 4
B_GROUP_DIM = B_WIDTH // B_GROUPS
C_HEADS = (MIX_WIDTH // 2) // HEAD_DIM
C_KV_HEADS = 2
C_GROUP = C_HEADS // C_KV_HEADS
WINDOW = 128
BLOCK = 128
D_HEADS = (MIX_WIDTH // 2) // HEAD_DIM
NA_ROWS_MAX = 8
NA_COLS = 16
N_EVEN = (DEPTH + 1) // 2
N_ODD = DEPTH // 2
EVEN_IN = 2 * A_WIDTH + B_WIDTH
C_Q = C_HEADS * HEAD_DIM
C_KV = C_KV_HEADS * HEAD_DIM
D_W = D_HEADS * HEAD_DIM
ODD_IN = C_Q + 2 * C_KV + 3 * D_W
ATTN_SCALE = HEAD_DIM ** -0.5

kernel_name = "hybrid_diffusion_prefix_trunk_step"


def rmsnorm(x, g):
    xf = x.astype(jnp.float32)
    y = xf * lax.rsqrt(jnp.mean(xf * xf, axis=-1, keepdims=True) + EPS)
    return (y * g.astype(jnp.float32)).astype(x.dtype)


def ada_mod(cvec, w, b):
    m = jax.nn.silu(cvec) @ w + b
    return m.reshape(cvec.shape[0], N_MOD, 1, D_MODEL)


def modulate(x, g, shift, scale):
    return rmsnorm(x, g) * (1 + scale) + shift


def swiglu(h, w_gu, w_down):
    gt, up = jnp.split(h @ w_gu, 2, axis=-1)
    return (jax.nn.silu(gt) * up) @ w_down


def axial_rope(x, n_tokens):
    t = jnp.arange(n_tokens)
    row = (t // GRID_W).astype(jnp.float32)
    col = (t % GRID_W).astype(jnp.float32)
    half = HEAD_DIM // 2
    nf = half // 2
    inv = ROPE_BASE ** (-jnp.arange(nf, dtype=jnp.float32) / nf)

    def rot(xp, pos):
        ang = pos[:, None] * inv[None, :]
        cos, sin = jnp.cos(ang), jnp.sin(ang)
        x1, x2 = xp[..., :nf], xp[..., nf:]
        return jnp.concatenate([x1 * cos - x2 * sin, x1 * sin + x2 * cos], axis=-1)

    xf = x.astype(jnp.float32)
    out = jnp.concatenate([rot(xf[..., :half], row), rot(xf[..., half:], col)], axis=-1)
    return out.astype(x.dtype)


def sink_softmax(s, sink):
    m = jnp.max(s, axis=-1, keepdims=True)
    if sink is not None:
        m = jnp.maximum(m, sink)
    p = jnp.exp(s - m)
    den = jnp.sum(p, axis=-1, keepdims=True)
    if sink is not None:
        den = den + jnp.exp(sink - m)
    return p / den


def ctx_attention(q, k, v, sink):
    B, Hk, G, S, hd = q.shape
    nb = S // BLOCK
    qb = jnp.moveaxis(q.reshape(B, Hk, G, nb, BLOCK, hd), 3, 0)
    sk = None if sink is None else sink.astype(jnp.float32)[None, :, :, None, None]

    def one(qi):
        s = jnp.einsum('bkgqd,bkld->bkgql', qi, k).astype(jnp.float32) * ATTN_SCALE
        p = sink_softmax(s, sk).astype(v.dtype)
        return jnp.einsum('bkgql,bkld->bkgqd', p, v)

    out = lax.map(one, qb)
    return jnp.moveaxis(out, 0, 3).reshape(B, Hk, G, S, hd)


def window_attention(q, k, v, k_ctx, v_ctx, sink):
    B, Hk, G, S, hd = q.shape
    L = k_ctx.shape[2]
    nb = S // BLOCK
    kp = jnp.pad(k, ((0, 0), (0, 0), (BLOCK, BLOCK), (0, 0)))
    vp = jnp.pad(v, ((0, 0), (0, 0), (BLOCK, BLOCK), (0, 0)))
    qpos = jnp.arange(BLOCK)
    kpos = jnp.arange(3 * BLOCK) - BLOCK
    rel_ok = jnp.abs(kpos[None, :] - qpos[:, None]) <= WINDOW
    sk = sink.astype(jnp.float32)[None, :, :, None, None]

    def one(j):
        qi = lax.dynamic_slice_in_dim(q, j * BLOCK, BLOCK, axis=3)
        kb = lax.dynamic_slice_in_dim(kp, j * BLOCK, 3 * BLOCK, axis=2)
        vb = lax.dynamic_slice_in_dim(vp, j * BLOCK, 3 * BLOCK, axis=2)
        kabs = j * BLOCK + kpos
        valid = rel_ok & ((kabs >= 0) & (kabs < S))[None, :]
        s_loc = jnp.einsum('bkgqd,bkld->bkgql', qi, kb).astype(jnp.float32) * ATTN_SCALE
        s_loc = jnp.where(valid, s_loc, NEG)
        s_ctx = jnp.einsum('bkgqd,bkld->bkgql', qi, k_ctx).astype(jnp.float32) * ATTN_SCALE
        p = sink_softmax(jnp.concatenate([s_ctx, s_loc], axis=-1), sk).astype(v.dtype)
        return (jnp.einsum('bkgql,bkld->bkgqd', p[..., :L], v_ctx)
                + jnp.einsum('bkgql,bkld->bkgqd', p[..., L:], vb))

    out = lax.map(one, jnp.arange(nb))
    return jnp.moveaxis(out, 0, 3).reshape(B, Hk, G, S, hd)


def neighborhood_attention(q, k, v, k_ctx, v_ctx, rpb):
    B, H, S, hd = q.shape
    rows = S // GRID_W
    kr = min(NA_ROWS_MAX, rows)
    kcn = min(NA_COLS, GRID_W)
    cols = jnp.arange(GRID_W)
    cstart = jnp.clip(cols - kcn // 2, 0, GRID_W - kcn)
    col_idx = cstart[:, None] + jnp.arange(kcn)[None, :]
    dc = col_idx - cols[:, None]
    q_grid = q.reshape(B, H, rows, GRID_W, hd)
    k_grid = k.reshape(B, H, rows, GRID_W, hd)
    v_grid = v.reshape(B, H, rows, GRID_W, hd)
    rpb32 = rpb.astype(jnp.float32)

    def one(r):
        rs = jnp.clip(r - kr // 2, 0, rows - kr)
        k_rows = lax.dynamic_slice_in_dim(k_grid, rs, kr, axis=2)
        v_rows = lax.dynamic_slice_in_dim(v_grid, rs, kr, axis=2)
        k_nb = k_rows[:, :, :, col_idx]
        v_nb = v_rows[:, :, :, col_idx]
        qr = lax.dynamic_index_in_dim(q_grid, r, axis=2, keepdims=False)
        dr = rs + jnp.arange(kr) - r
        bias = rpb32[:, (dr + NA_ROWS_MAX - 1)[None, :, None], (dc + NA_COLS - 1)[:, None, :]]
        s_nb = jnp.einsum('bhqd,bhrqcd->bhqrc', qr, k_nb).astype(jnp.float32) * ATTN_SCALE + bias[None]
        s_nb = s_nb.reshape(B, H, GRID_W, kr * kcn)
        s_ctx = jnp.einsum('bhqd,bhld->bhql', qr, k_ctx).astype(jnp.float32) * ATTN_SCALE
        L = s_ctx.shape[-1]
        p = sink_softmax(jnp.concatenate([s_ctx, s_nb], axis=-1), None).astype(v.dtype)
        p_nb = p[..., L:].reshape(B, H, GRID_W, kr, kcn)
        return (jnp.einsum('bhql,bhld->bhqd', p[..., :L], v_ctx)
                + jnp.einsum('bhqrc,bhrqcd->bhqd', p_nb, v_nb))

    out = lax.map(one, jnp.arange(rows))
    return jnp.moveaxis(out, 0, 2).reshape(B, H, S, hd)


def even_mixer(h, w_in, g_sgu, w_sgu, b_sgu):
    B, S, _ = h.shape
    u, v, f = jnp.split(h @ w_in, [A_WIDTH, 2 * A_WIDTH], axis=-1)
    v = rmsnorm(v, g_sgu)
    nch = S // CHUNK
    vg = v.reshape(B, nch, CHUNK, A_GROUPS, A_GROUP_DIM)
    sp = jnp.einsum('gpq,bnqgc->bnpgc', w_sgu, vg) + b_sgu.T[None, None, :, :, None]
    a_out = u * sp.reshape(B, S, A_WIDTH)
    fb = f.reshape(B, S, B_GROUPS, B_GROUP_DIM).astype(jnp.float32)
    b_out = jnp.real(jnp.fft.fft2(fb, axes=(1, 3), norm='ortho')).astype(h.dtype).reshape(B, S, B_WIDTH)
    return jnp.concatenate([a_out, b_out], axis=-1)


def odd_heads(h, w_in, g_qc, g_kc, g_qd, g_kd):
    B, S, _ = h.shape
    qc, kc, vc, qd, kd, vd = jnp.split(
        h @ w_in, [C_Q, C_Q + C_KV, C_Q + 2 * C_KV, C_Q + 2 * C_KV + D_W, C_Q + 2 * C_KV + 2 * D_W], axis=-1)

    def heads(t, n):
        return t.reshape(B, S, n, HEAD_DIM).transpose(0, 2, 1, 3)

    return (rmsnorm(heads(qc, C_HEADS), g_qc), rmsnorm(heads(kc, C_KV_HEADS), g_kc), heads(vc, C_KV_HEADS),
            rmsnorm(heads(qd, D_HEADS), g_qd), rmsnorm(heads(kd, D_HEADS), g_kd), heads(vd, D_HEADS))


def merge_heads(oc, od):
    o = jnp.concatenate([oc, od], axis=1)
    B, H, S, hd = o.shape
    return o.transpose(0, 2, 1, 3).reshape(B, S, H * hd)


def odd_context(h, w_in, g_qc, g_kc, g_qd, g_kd, sink):
    B, S, _ = h.shape
    qc, kc, vc, qd, kd, vd = odd_heads(h, w_in, g_qc, g_kc, g_qd, g_kd)
    oc = ctx_attention(qc.reshape(B, C_KV_HEADS, C_GROUP, S, HEAD_DIM), kc, vc, sink.reshape(C_KV_HEADS, C_GROUP))
    od = ctx_attention(qd[:, :, None], kd, vd, None)
    return merge_heads(oc.reshape(B, C_HEADS, S, HEAD_DIM), od.reshape(B, D_HEADS, S, HEAD_DIM)), kc, vc, kd, vd


def odd_latent(h, k_ctx_c, v_ctx_c, k_ctx_d, v_ctx_d, w_in, g_qc, g_kc, g_qd, g_kd, sink, rpb):
    B, S, _ = h.shape
    qc, kc, vc, qd, kd, vd = odd_heads(h, w_in, g_qc, g_kc, g_qd, g_kd)
    qc = axial_rope(qc, S)
    kc = axial_rope(kc, S)
    oc = window_attention(qc.reshape(B, C_KV_HEADS, C_GROUP, S, HEAD_DIM), kc, vc, k_ctx_c, v_ctx_c,
                          sink.reshape(C_KV_HEADS, C_GROUP))
    od = neighborhood_attention(qd, kd, vd, k_ctx_d, v_ctx_d, rpb)
    return merge_heads(oc.reshape(B, C_HEADS, S, HEAD_DIM), od)


def setup_inputs(seed: int = 0) -> dict:
    key = jax.random.key(seed)
    ks = jax.random.split(key, 32)

    def nrm(k, shape, s):
        return jax.random.normal(k, shape, jnp.float32) * s

    def gain(k, shape):
        return 1.0 + 0.02 * jax.random.normal(k, shape, jnp.float32)

    return {
        "x_prompt": nrm(ks[0], (BATCH, SEQ, D_MODEL), 1.0),
        "x_sample": nrm(ks[1], (DEC_BATCH, DEC_SEQ, D_MODEL), 1.0),
        "c": nrm(ks[2], (DEC_BATCH, D_MODEL), 1.0),
        "cache_c_k": nrm(ks[3], (DEC_BATCH, N_ODD, C_KV_HEADS, PAST_LEN, HEAD_DIM), 1.0),
        "cache_c_v": nrm(ks[4], (DEC_BATCH, N_ODD, C_KV_HEADS, PAST_LEN, HEAD_DIM), 1.0),
        "cache_d_k": nrm(ks[5], (DEC_BATCH, N_ODD, D_HEADS, PAST_LEN, HEAD_DIM), 1.0),
        "cache_d_v": nrm(ks[6], (DEC_BATCH, N_ODD, D_HEADS, PAST_LEN, HEAD_DIM), 1.0),
        "c_ctx": nrm(ks[7], (D_MODEL,), 1.0),
        "w_ada": nrm(ks[8], (DEPTH, D_MODEL, N_MOD * D_MODEL), 0.5 * D_MODEL ** -0.5),
        "b_ada": nrm(ks[9], (DEPTH, N_MOD * D_MODEL), 0.02),
        "g_ffn1": gain(ks[10], (DEPTH, D_MODEL)),
        "w_ffn1_gu": nrm(ks[11], (DEPTH, D_MODEL, 2 * D_FF), D_MODEL ** -0.5),
        "w_ffn1_down": nrm(ks[12], (DEPTH, D_FF, D_MODEL), D_FF ** -0.5),
        "g_mix": gain(ks[13], (DEPTH, D_MODEL)),
        "w_mix_out": nrm(ks[14], (DEPTH, MIX_WIDTH, D_MODEL), MIX_WIDTH ** -0.5),
        "g_ffn2": gain(ks[15], (DEPTH, D_MODEL)),
        "w_ffn2_gu": nrm(ks[16], (DEPTH, D_MODEL, 2 * D_FF), D_MODEL ** -0.5),
        "w_ffn2_down": nrm(ks[17], (DEPTH, D_FF, D_MODEL), D_FF ** -0.5),
        "w_in_even": nrm(ks[18], (N_EVEN, D_MODEL, EVEN_IN), D_MODEL ** -0.5),
        "g_sgu": gain(ks[19], (N_EVEN, A_WIDTH)),
        "w_sgu": nrm(ks[20], (N_EVEN, A_GROUPS, CHUNK, CHUNK), CHUNK ** -0.5),
        "b_sgu": gain(ks[21], (N_EVEN, A_GROUPS, CHUNK)),
        "w_in_odd": nrm(ks[22], (N_ODD, D_MODEL, ODD_IN), D_MODEL ** -0.5),
        "g_q_c": gain(ks[23], (N_ODD, HEAD_DIM)),
        "g_k_c": gain(ks[24], (N_ODD, HEAD_DIM)),
        "g_q_d": gain(ks[25], (N_ODD, HEAD_DIM)),
        "g_k_d": gain(ks[26], (N_ODD, HEAD_DIM)),
        "sink_c": nrm(ks[27], (N_ODD, C_HEADS), 0.5),
        "rpb_d": nrm(ks[28], (N_ODD, D_HEADS, 2 * NA_ROWS_MAX - 1, 2 * NA_COLS - 1), 0.1),
    }


def reference(x_prompt, x_sample, c, cache_c_k, cache_c_v, cache_d_k, cache_d_v, c_ctx,
              w_ada, b_ada, g_ffn1, w_ffn1_gu, w_ffn1_down, g_mix, w_mix_out, g_ffn2, w_ffn2_gu, w_ffn2_down,
              w_in_even, g_sgu, w_sgu, b_sgu, w_in_odd, g_q_c, g_k_c, g_q_d, g_k_d, sink_c, rpb_d):
    xp = x_prompt
    xs = x_sample
    ck_list, cv_list, dk_list, dv_list = [], [], [], []
    for layer in range(DEPTH):
        mc = ada_mod(c_ctx[None, :], w_ada[layer], b_ada[layer])
        ml = ada_mod(c, w_ada[layer], b_ada[layer])
        xp = xp + 0.5 * mc[:, 2] * swiglu(modulate(xp, g_ffn1[layer], mc[:, 0], mc[:, 1]),
                                          w_ffn1_gu[layer], w_ffn1_down[layer])
        xs = xs + 0.5 * ml[:, 2] * swiglu(modulate(xs, g_ffn1[layer], ml[:, 0], ml[:, 1]),
                                          w_ffn1_gu[layer], w_ffn1_down[layer])
        hp = modulate(xp, g_mix[layer], mc[:, 3], mc[:, 4])
        hs = modulate(xs, g_mix[layer], ml[:, 3], ml[:, 4])
        if layer % 2 == 0:
            e = layer // 2
            op = even_mixer(hp, w_in_even[e], g_sgu[e], w_sgu[e], b_sgu[e])
            os_ = even_mixer(hs, w_in_even[e], g_sgu[e], w_sgu[e], b_sgu[e])
        else:
            o = layer // 2
            op, kc, vc, kd, vd = odd_context(hp, w_in_odd[o], g_q_c[o], g_k_c[o], g_q_d[o], g_k_d[o], sink_c[o])
            ck_list.append(kc)
            cv_list.append(vc)
            dk_list.append(kd)
            dv_list.append(vd)
            os_ = odd_latent(hs, cache_c_k[:, o], cache_c_v[:, o], cache_d_k[:, o], cache_d_v[:, o],
                             w_in_odd[o], g_q_c[o], g_k_c[o], g_q_d[o], g_k_d[o], sink_c[o], rpb_d[o])
        xp = xp + mc[:, 5] * (op @ w_mix_out[layer])
        xs = xs + ml[:, 5] * (os_ @ w_mix_out[layer])
        xp = xp + 0.5 * mc[:, 8] * swiglu(modulate(xp, g_ffn2[layer], mc[:, 6], mc[:, 7]),
                                          w_ffn2_gu[layer], w_ffn2_down[layer])
        xs = xs + 0.5 * ml[:, 8] * swiglu(modulate(xs, g_ffn2[layer], ml[:, 6], ml[:, 7]),
                                          w_ffn2_gu[layer], w_ffn2_down[layer])
    state_c_k = jnp.stack(ck_list, axis=1)
    state_c_v = jnp.stack(cv_list, axis=1)
    state_d_k = jnp.stack(dk_list, axis=1)
    state_d_v = jnp.stack(dv_list, axis=1)
    return (xp, xs, state_c_k, state_c_v, state_d_k, state_d_v)
```

```python
import functools

import numpy as np
import jax
import jax.numpy as jnp
from jax import lax
from jax.experimental import pallas as pl
from jax.experimental.pallas import tpu as pltpu

D_MODEL = 2048
BATCH = 32
SEQ = 256
DEPTH = 4
DEC_BATCH = 2
DEC_SEQ = 1024
PAST_LEN = 512
GRID_W = 64
HEAD_DIM = 128
D_FF = 5504
N_MOD = 9
EPS = 1e-6
NEG = -1e30
ROPE_BASE = 10000.0
CHUNK = 128
A_WIDTH = 1024
A_GROUPS = 8
B_WIDTH = 1024
B_GROUPS = 4
B_GROUP_DIM = 256
C_HEADS = 8
C_KV_HEADS = 2
C_GROUP = 4
WINDOW = 128
BLOCK = 128
D_HEADS = 8
NA_ROWS = 8
NA_COLS = 16
N_ODD = 2
EVEN_IN = 3072
ODD_IN = 4608
ATTN_SCALE = HEAD_DIM ** -0.5

N_PROMPT = BATCH * SEQ
N_SAMPLE = DEC_BATCH * DEC_SEQ
N_TOK = N_PROMPT + N_SAMPLE
N_GROUPS = 1 + DEC_BATCH
ADA_ROWS = 8

OFF_QC, OFF_KC, OFF_VC = 0, 1024, 1280
OFF_QD, OFF_KD, OFF_VD = 1536, 2560, 3584

LANE = 128
MXU_DIM = 256
VMEM_LIMIT = 56 * 1024 * 1024

TOK_TILE = 512
FF_TILE = 512
D_FF_PAD = -(-D_FF // FF_TILE) * FF_TILE
IN_TILE = 1536
ADA_TILE = 1024

BF16 = jnp.bfloat16
F32 = jnp.float32


def _params(*sem):
    return pltpu.CompilerParams(dimension_semantics=sem, vmem_limit_bytes=VMEM_LIMIT)


def _dot(a, b):
    return jnp.dot(a, b, preferred_element_type=F32)


def _dot_t(a, b):
    return lax.dot_general(a, b, (((1,), (1,)), ((), ())), preferred_element_type=F32)


def _rms(x, g):
    n = x.shape[-1]
    return (x * lax.rsqrt(jnp.sum(x * x, axis=-1, keepdims=True) * (1.0 / n) + EPS)) * g


def _modulate(x, g, shift, scale):
    return _rms(x, g) * (1 + scale) + shift


def _tile_group(i, tile):
    n_prompt_tiles = N_PROMPT // tile
    per_batch = DEC_SEQ // tile
    return jnp.where(i < n_prompt_tiles, 0, 1 + (i - n_prompt_tiles) // per_batch)


def _ada_kernel(c_ref, w_ref, b_ref, o_ref):
    s = jax.nn.silu(c_ref[...]).astype(BF16)
    o_ref[...] = _dot(s, w_ref[...].astype(BF16)) + b_ref[...]


def _ada_mod(cvec, w_ada, b_ada):
    n_out = N_MOD * D_MODEL
    return pl.pallas_call(
        _ada_kernel,
        grid=(DEPTH, n_out // ADA_TILE),
        in_specs=[
            pl.BlockSpec((ADA_ROWS, D_MODEL), lambda l, j: (0, 0)),
            pl.BlockSpec((None, D_MODEL, ADA_TILE), lambda l, j: (l, 0, j)),
            pl.BlockSpec((None, 1, ADA_TILE), lambda l, j: (l, 0, j)),
        ],
        out_specs=pl.BlockSpec((None, ADA_ROWS, ADA_TILE), lambda l, j: (l, 0, j)),
        out_shape=jax.ShapeDtypeStruct((DEPTH, ADA_ROWS, n_out), F32),
        compiler_params=_params("parallel", "parallel"),
        name="ada_mod",
    )(cvec, w_ada, b_ada.reshape(DEPTH, 1, n_out))


def _ffn_kernel(x_ref, g_ref, mod_ref, wg_ref, wu_ref, wd_ref, o_ref, h_ref, acc_ref, *, first):
    j = pl.program_id(1)

    @pl.when(j == 0)
    def _():
        h = _modulate(x_ref[...], g_ref[...], mod_ref[first:first + 1, :], mod_ref[first + 1:first + 2, :])
        h_ref[...] = h.astype(BF16)
        acc_ref[...] = jnp.zeros_like(acc_ref)

    h = h_ref[...]
    gate = _dot(h, wg_ref[...])
    up = _dot(h, wu_ref[...])
    a = (jax.nn.silu(gate) * up).astype(BF16)
    acc_ref[...] += _dot(a, wd_ref[...])

    @pl.when(j == pl.num_programs(1) - 1)
    def _():
        o_ref[...] = x_ref[...] + (0.5 * mod_ref[first + 2:first + 3, :]) * acc_ref[...]


def _ffn(x, g, mod, wg, wu, wd, first):
    tm, tf = TOK_TILE, FF_TILE
    return pl.pallas_call(
        functools.partial(_ffn_kernel, first=first),
        grid=(N_TOK // tm, D_FF_PAD // tf),
        in_specs=[
            pl.BlockSpec((tm, D_MODEL), lambda i, j: (i, 0)),
            pl.BlockSpec((1, D_MODEL), lambda i, j: (0, 0)),
            pl.BlockSpec((None, N_MOD, D_MODEL), lambda i, j: (_tile_group(i, tm), 0, 0)),
            pl.BlockSpec((D_MODEL, tf), lambda i, j: (0, j)),
            pl.BlockSpec((D_MODEL, tf), lambda i, j: (0, j)),
            pl.BlockSpec((tf, D_MODEL), lambda i, j: (j, 0)),
        ],
        out_specs=pl.BlockSpec((tm, D_MODEL), lambda i, j: (i, 0)),
        out_shape=jax.ShapeDtypeStruct((N_TOK, D_MODEL), F32),
        scratch_shapes=[pltpu.VMEM((tm, D_MODEL), BF16), pltpu.VMEM((tm, D_MODEL), F32)],
        compiler_params=_params("parallel", "arbitrary"),
        name="ffn",
    )(x, g.reshape(1, D_MODEL), mod, wg, wu, wd)


def _ffn_weights(w_gu, w_down):
    pad = D_FF_PAD - D_FF
    wg = jnp.pad(w_gu[:, :D_FF], ((0, 0), (0, pad))).astype(BF16)
    wu = jnp.pad(w_gu[:, D_FF:], ((0, 0), (0, pad))).astype(BF16)
    wd = jnp.pad(w_down, ((0, pad), (0, 0))).astype(BF16)
    return wg, wu, wd


def _inproj_kernel(x_ref, g_ref, mod_ref, w_ref, o_ref):
    h = _modulate(x_ref[...], g_ref[...], mod_ref[3:4, :], mod_ref[4:5, :])
    o_ref[...] = _dot(h.astype(BF16), w_ref[...])


def _inproj(x, g, mod, w):
    tm, tn = TOK_TILE, IN_TILE
    n_out = w.shape[1]
    return pl.pallas_call(
        _inproj_kernel,
        grid=(n_out // tn, N_TOK // tm),
        in_specs=[
            pl.BlockSpec((tm, D_MODEL), lambda j, i: (i, 0)),
            pl.BlockSpec((1, D_MODEL), lambda j, i: (0, 0)),
            pl.BlockSpec((None, N_MOD, D_MODEL), lambda j, i: (_tile_group(i, tm), 0, 0)),
            pl.BlockSpec((D_MODEL, tn), lambda j, i: (0, j)),
        ],
        out_specs=pl.BlockSpec((tm, tn), lambda j, i: (i, j)),
        out_shape=jax.ShapeDtypeStruct((N_TOK, n_out), F32),
        compiler_params=_params("parallel", "parallel"),
        name="inproj",
    )(x, g.reshape(1, D_MODEL), mod, w)


def _outproj_kernel(x_ref, mod_ref, op_ref, oa_ref, ob_ref, w_ref, o_ref, *, n_prompt_tiles):
    i = pl.program_id(0)
    gate = mod_ref[5:6, :]

    @pl.when(i < n_prompt_tiles)
    def _():
        o_ref[...] = x_ref[...] + gate * _dot(op_ref[...], w_ref[...])

    @pl.when(i >= n_prompt_tiles)
    def _():
        half = D_MODEL // 2
        y = _dot(oa_ref[...], w_ref[:half, :]) + _dot(ob_ref[...], w_ref[half:, :])
        o_ref[...] = x_ref[...] + gate * y


def _outproj(x, mod, o_prompt, o_sample_a, o_sample_b, w):
    tm = TOK_TILE
    npt = N_PROMPT // tm
    half = D_MODEL // 2
    return pl.pallas_call(
        functools.partial(_outproj_kernel, n_prompt_tiles=npt),
        grid=(N_TOK // tm,),
        in_specs=[
            pl.BlockSpec((tm, D_MODEL), lambda i: (i, 0)),
            pl.BlockSpec((None, N_MOD, D_MODEL), lambda i: (_tile_group(i, tm), 0, 0)),
            pl.BlockSpec((tm, D_MODEL), lambda i: (jnp.minimum(i, npt - 1), 0)),
            pl.BlockSpec((tm, half), lambda i: (jnp.maximum(i - npt, 0), 0)),
            pl.BlockSpec((tm, half), lambda i: (jnp.maximum(i - npt, 0), 0)),
            pl.BlockSpec((D_MODEL, D_MODEL), lambda i: (0, 0)),
        ],
        out_specs=pl.BlockSpec((tm, D_MODEL), lambda i: (i, 0)),
        out_shape=jax.ShapeDtypeStruct((N_TOK, D_MODEL), F32),
        compiler_params=_params("parallel"),
        name="outproj",
    )(x, mod, o_prompt, o_sample_a, o_sample_b, w)


def _dft_tables(n):
    k = np.arange(n)
    ang = 2.0 * np.pi * ((k[:, None] * k[None, :]) % n) / n
    return np.cos(ang) / np.sqrt(n), np.sin(ang) / np.sqrt(n)


def _even_kernel(x_ref, v_ref, gs_ref, ws_ref, bs_ref, cch_ref, cpos_ref, *out_refs, seq):
    if len(out_refs) == 1:
        a_ref, a_off, b_ref, b_off = out_refs[0], 0, out_refs[0], A_WIDTH
    else:
        a_ref, a_off, b_ref, b_off = out_refs[0], 0, out_refs[1], 0
    part = pl.program_id(1)

    @pl.when(part == 0)
    def _():
        nch = seq // CHUNK
        vn = _rms(v_ref[...], gs_ref[...]).astype(BF16)
        for g in range(A_GROUPS):
            cols = slice(g * CHUNK, (g + 1) * CHUNK)
            rhs = jnp.concatenate([vn[n * CHUNK:(n + 1) * CHUNK, cols] for n in range(nch)], axis=1)
            sp = _dot(ws_ref[g], rhs) + bs_ref[:, g:g + 1]
            for n in range(nch):
                rows = slice(n * CHUNK, (n + 1) * CHUNK)
                a = x_ref[rows, cols] * sp[:, n * CHUNK:(n + 1) * CHUNK]
                a_ref[rows, a_off + g * CHUNK:a_off + (g + 1) * CHUNK] = a.astype(BF16)

    @pl.when(part == 1)
    def _():
        for g in range(B_GROUPS):
            cols = slice(g * B_GROUP_DIM, (g + 1) * B_GROUP_DIM)
            y = _dot(x_ref[:, cols].astype(BF16), cch_ref[...])
            y = jnp.concatenate([y[:, :B_GROUP_DIM], y[:, B_GROUP_DIM:]], axis=0)
            out = _dot(cpos_ref[...], y.astype(BF16))
            b_ref[:, b_off + g * B_GROUP_DIM:b_off + (g + 1) * B_GROUP_DIM] = out.astype(BF16)


def _even_mixer(proj, g_sgu, w_sgu, b_sgu, *, seq, n_seq, row0, split):
    cc, sc = _dft_tables(B_GROUP_DIM)
    cp, sp = _dft_tables(seq)
    cch = jnp.asarray(np.concatenate([cc, sc], axis=1), BF16)
    cpos = jnp.asarray(np.concatenate([cp, -sp], axis=1), BF16)
    blk0 = row0 // seq
    n_rows = n_seq * seq
    if split:
        out_shape = [jax.ShapeDtypeStruct((n_rows, A_WIDTH), BF16), jax.ShapeDtypeStruct((n_rows, B_WIDTH), BF16)]
        out_specs = [pl.BlockSpec((seq, A_WIDTH), lambda b, p: (b, 0)), pl.BlockSpec((seq, B_WIDTH), lambda b, p: (b, 0))]
    else:
        out_shape = jax.ShapeDtypeStruct((n_rows, A_WIDTH + B_WIDTH), BF16)
        out_specs = pl.BlockSpec((seq, A_WIDTH + B_WIDTH), lambda b, p: (b, 0))
    const2 = lambda b, p: (0, 0)
    return pl.pallas_call(
        functools.partial(_even_kernel, seq=seq),
        grid=(n_seq, 2),
        in_specs=[
            pl.BlockSpec((seq, A_WIDTH), lambda b, p: (blk0 + b, 2 * p)),
            pl.BlockSpec((seq, A_WIDTH), lambda b, p: (blk0 + b, 1)),
            pl.BlockSpec((1, A_WIDTH), const2),
            pl.BlockSpec((A_GROUPS, CHUNK, CHUNK), lambda b, p: (0, 0, 0)),
            pl.BlockSpec((CHUNK, A_GROUPS), const2),
            pl.BlockSpec((B_GROUP_DIM, 2 * B_GROUP_DIM), const2),
            pl.BlockSpec((seq, 2 * seq), const2),
        ],
        out_specs=out_specs,
        out_shape=out_shape,
        compiler_params=_params("parallel", "arbitrary"),
        name="even_mixer",
    )(proj, proj, g_sgu.reshape(1, A_WIDTH), w_sgu.astype(BF16), b_sgu.T, cch, cpos)


def _softmax_rows(scores, sink):
    m = functools.reduce(jnp.maximum, [jnp.max(s, axis=-1, keepdims=True) for s in scores])
    if sink is not None:
        m = jnp.maximum(m, sink)
    ps = [jnp.exp(s - m) for s in scores]
    den = functools.reduce(lambda a, b: a + b, [jnp.sum(p, axis=-1, keepdims=True) for p in ps])
    if sink is not None:
        den = den + jnp.exp(sink - m)
    inv = 1.0 / den
    return [(p * inv).astype(BF16) for p in ps]


def _odd_ctx_kernel(sink_ref, p_ref, gqc_ref, gkc_ref, gqd_ref, gkd_ref, o_ref, ck_ref, cv_ref, dk_ref, dv_ref):
    hd = HEAD_DIM

    def col(off, h):
        return p_ref[:, off + h * hd:off + (h + 1) * hd]

    for k in range(C_KV_HEADS):
        kn = _rms(col(OFF_KC, k), gkc_ref[...])
        v = col(OFF_VC, k)
        ck_ref[k] = kn
        cv_ref[k] = v
        q = jnp.concatenate([_rms(col(OFF_QC, k * C_GROUP + g), gqc_ref[...]).astype(BF16) for g in range(C_GROUP)], axis=0)
        s = _dot_t(q, kn.astype(BF16)) * ATTN_SCALE
        sink = jnp.concatenate([jnp.full((SEQ, 1), sink_ref[k * C_GROUP + g], F32) for g in range(C_GROUP)], axis=0)
        (p,) = _softmax_rows([s], sink)
        o = _dot(p, v.astype(BF16))
        for g in range(C_GROUP):
            h = k * C_GROUP + g
            o_ref[:, h * hd:(h + 1) * hd] = o[g * SEQ:(g + 1) * SEQ].astype(BF16)

    for h in range(D_HEADS):
        qn = _rms(col(OFF_QD, h), gqd_ref[...]).astype(BF16)
        kn = _rms(col(OFF_KD, h), gkd_ref[...])
        v = col(OFF_VD, h)
        dk_ref[h] = kn
        dv_ref[h] = v
        s = _dot_t(qn, kn.astype(BF16)) * ATTN_SCALE
        (p,) = _softmax_rows([s], None)
        o_ref[:, (C_HEADS + h) * hd:(C_HEADS + h + 1) * hd] = _dot(p, v.astype(BF16)).astype(BF16)


def _odd_context(proj, g_qc, g_kc, g_qd, g_kd, sink):
    gspec = pl.BlockSpec((1, HEAD_DIM), lambda b: (0, 0))
    g2 = lambda g: g.reshape(1, HEAD_DIM)
    state = lambda heads: jax.ShapeDtypeStruct((BATCH, heads, SEQ, HEAD_DIM), F32)
    sspec = lambda heads: pl.BlockSpec((None, heads, SEQ, HEAD_DIM), lambda b: (b, 0, 0, 0))
    return pl.pallas_call(
        _odd_ctx_kernel,
        grid=(BATCH,),
        in_specs=[
            pl.BlockSpec(memory_space=pltpu.SMEM),
            pl.BlockSpec((SEQ, ODD_IN), lambda b: (b, 0)),
            gspec, gspec, gspec, gspec,
        ],
        out_specs=[pl.BlockSpec((SEQ, D_MODEL), lambda b: (b, 0)),
                   sspec(C_KV_HEADS), sspec(C_KV_HEADS), sspec(D_HEADS), sspec(D_HEADS)],
        out_shape=[jax.ShapeDtypeStruct((N_PROMPT, D_MODEL), BF16),
                   state(C_KV_HEADS), state(C_KV_HEADS), state(D_HEADS), state(D_HEADS)],
        compiler_params=_params("parallel"),
        name="odd_context",
    )(sink, proj, g2(g_qc), g2(g_kc), g2(g_qd), g2(g_kd))


def _rope_tables():
    t = jnp.arange(DEC_SEQ)
    row = (t // GRID_W).astype(F32)
    col = (t % GRID_W).astype(F32)
    nf = HEAD_DIM // 4
    inv = ROPE_BASE ** (-jnp.arange(nf, dtype=F32) / nf)
    ar = row[:, None] * inv[None, :]
    ac = col[:, None] * inv[None, :]
    cos = jnp.concatenate([jnp.cos(ar), jnp.cos(ar), jnp.cos(ac), jnp.cos(ac)], axis=-1)
    sin = jnp.concatenate([-jnp.sin(ar), jnp.sin(ar), -jnp.sin(ac), jnp.sin(ac)], axis=-1)
    return cos, sin


def _rope(x, cos, sin):
    nf = HEAD_DIM // 4
    lane = lax.broadcasted_iota(jnp.int32, x.shape, 1)
    partner = jnp.where((lane & (2 * nf - 1)) < nf,
                        pltpu.roll(x, HEAD_DIM - nf, 1),
                        pltpu.roll(x, nf, 1))
    return x * cos + partner * sin


def _lat_c_kernel(sink_ref, q_ref, k_ref, v_ref, kc_ref, vc_ref, cos_ref, sin_ref, gq_ref, gk_ref, o_ref):
    kv = pl.program_id(1)
    hd = HEAD_DIM
    cos, sin = cos_ref[...], sin_ref[...]
    kn = _rope(_rms(k_ref[...], gk_ref[...]), cos, sin).astype(BF16)
    vb = v_ref[...].astype(BF16)
    kctx = kc_ref[...].astype(BF16)
    vctx = vc_ref[...].astype(BF16)
    qh = [_rope(_rms(q_ref[:, g * hd:(g + 1) * hd], gq_ref[...]), cos, sin).astype(BF16) for g in range(C_GROUP)]
    nb = DEC_SEQ // BLOCK
    for j in range(nb):
        lo, hi = max(0, j - 1) * BLOCK, min(nb, j + 2) * BLOCK
        rows = slice(j * BLOCK, (j + 1) * BLOCK)
        q = jnp.concatenate([qh[g][rows] for g in range(C_GROUP)], axis=0)
        s_loc = _dot_t(q, kn[lo:hi]) * ATTN_SCALE
        qpos = j * BLOCK + (lax.broadcasted_iota(jnp.int32, s_loc.shape, 0) & (BLOCK - 1))
        kpos = lo + lax.broadcasted_iota(jnp.int32, s_loc.shape, 1)
        s_loc = jnp.where(jnp.abs(kpos - qpos) <= WINDOW, s_loc, NEG)
        s_ctx = _dot_t(q, kctx) * ATTN_SCALE
        sink = jnp.concatenate([jnp.full((BLOCK, 1), sink_ref[kv * C_GROUP + g], F32) for g in range(C_GROUP)], axis=0)
        p_ctx, p_loc = _softmax_rows([s_ctx, s_loc], sink)
        o = _dot(p_ctx, vctx) + _dot(p_loc, vb[lo:hi])
        for g in range(C_GROUP):
            o_ref[rows, g * hd:(g + 1) * hd] = o[g * BLOCK:(g + 1) * BLOCK].astype(BF16)


def _odd_latent_c(proj, cache_k, cache_v, layer_o, g_qc, g_kc, sink):
    row_blk0 = N_PROMPT // DEC_SEQ
    qw = C_GROUP * HEAD_DIM
    cos, sin = _rope_tables()
    gspec = pl.BlockSpec((1, HEAD_DIM), lambda b, k: (0, 0))
    tspec = pl.BlockSpec((DEC_SEQ, HEAD_DIM), lambda b, k: (0, 0))
    cspec = pl.BlockSpec((None, None, None, PAST_LEN, HEAD_DIM), lambda b, k: (b, layer_o, k, 0, 0))
    return pl.pallas_call(
        _lat_c_kernel,
        grid=(DEC_BATCH, C_KV_HEADS),
        in_specs=[
            pl.BlockSpec(memory_space=pltpu.SMEM),
            pl.BlockSpec((DEC_SEQ, qw), lambda b, k: (row_blk0 + b, k)),
            pl.BlockSpec((DEC_SEQ, HEAD_DIM), lambda b, k: (row_blk0 + b, OFF_KC // HEAD_DIM + k)),
            pl.BlockSpec((DEC_SEQ, HEAD_DIM), lambda b, k: (row_blk0 + b, OFF_VC // HEAD_DIM + k)),
            cspec, cspec, tspec, tspec, gspec, gspec,
        ],
        out_specs=pl.BlockSpec((DEC_SEQ, qw), lambda b, k: (b, k)),
        out_shape=jax.ShapeDtypeStruct((N_SAMPLE, C_HEADS * HEAD_DIM), BF16),
        compiler_params=_params("parallel", "parallel"),
        name="odd_latent_c",
    )(sink, proj, proj, proj, cache_k, cache_v, cos, sin, g_qc.reshape(1, HEAD_DIM), g_kc.reshape(1, HEAD_DIM))


def _na_row_start(r):
    rows = DEC_SEQ // GRID_W
    return min(max(r - NA_ROWS // 2, 0), rows - NA_ROWS)


def _na_bias_table(rpb):
    qc = np.arange(GRID_W)
    cstart = np.clip(qc - NA_COLS // 2, 0, GRID_W - NA_COLS)
    kc = np.arange(GRID_W)
    valid = (kc[None, :] >= cstart[:, None]) & (kc[None, :] < cstart[:, None] + NA_COLS)
    dc_idx = np.clip(kc[None, :] - qc[:, None] + NA_COLS - 1, 0, 2 * NA_COLS - 2)
    dr_idx = np.arange(NA_ROWS)[:, None] + np.arange(NA_ROWS)[None, :]
    tab = rpb.astype(F32)[:, dr_idx[:, :, None, None], dc_idx[None, None, :, :]]
    tab = jnp.where(valid[None, None, None], tab, NEG)
    return tab.transpose(0, 1, 3, 2, 4).reshape(D_HEADS, NA_ROWS, GRID_W, NA_ROWS * GRID_W)


def _lat_d_kernel(q_ref, k_ref, v_ref, kc_ref, vc_ref, tab_ref, gq_ref, gk_ref, o_ref):
    qn = _rms(q_ref[...], gq_ref[...]).astype(BF16)
    kn = _rms(k_ref[...], gk_ref[...]).astype(BF16)
    vb = v_ref[...].astype(BF16)
    kctx = kc_ref[...].astype(BF16)
    vctx = vc_ref[...].astype(BF16)
    span = NA_ROWS * GRID_W
    for r in range(DEC_SEQ // GRID_W):
        rs = _na_row_start(r)
        rows = slice(r * GRID_W, (r + 1) * GRID_W)
        keys = slice(rs * GRID_W, rs * GRID_W + span)
        q = qn[rows]
        s_nb = _dot_t(q, kn[keys]) * ATTN_SCALE + tab_ref[rs - r + NA_ROWS - 1]
        s_ctx = _dot_t(q, kctx) * ATTN_SCALE
        p_ctx, p_nb = _softmax_rows([s_ctx, s_nb], None)
        o_ref[rows, :] = (_dot(p_ctx, vctx) + _dot(p_nb, vb[keys])).astype(BF16)


def _odd_latent_d(proj, cache_k, cache_v, layer_o, g_qd, g_kd, rpb):
    row_blk0 = N_PROMPT // DEC_SEQ
    tab = _na_bias_table(rpb)
    gspec = pl.BlockSpec((1, HEAD_DIM), lambda b, h: (0, 0))
    cspec = pl.BlockSpec((None, None, None, PAST_LEN, HEAD_DIM), lambda b, h: (b, layer_o, h, 0, 0))
    pspec = lambda off: pl.BlockSpec((DEC_SEQ, HEAD_DIM), lambda b, h: (row_blk0 + b, off // HEAD_DIM + h))
    return pl.pallas_call(
        _lat_d_kernel,
        grid=(DEC_BATCH, D_HEADS),
        in_specs=[
            pspec(OFF_QD), pspec(OFF_KD), pspec(OFF_VD), cspec, cspec,
            pl.BlockSpec((None, NA_ROWS, GRID_W, NA_ROWS * GRID_W), lambda b, h: (h, 0, 0, 0)),
            gspec, gspec,
        ],
        out_specs=pl.BlockSpec((DEC_SEQ, HEAD_DIM), lambda b, h: (b, h)),
        out_shape=jax.ShapeDtypeStruct((N_SAMPLE, D_HEADS * HEAD_DIM), BF16),
        compiler_params=_params("parallel", "parallel"),
        name="odd_latent_d",
    )(proj, proj, proj, cache_k, cache_v, tab, g_qd.reshape(1, HEAD_DIM), g_kd.reshape(1, HEAD_DIM))


def kernel(x_prompt, x_sample, c, cache_c_k, cache_c_v, cache_d_k, cache_d_v, c_ctx, w_ada, b_ada, g_ffn1, w_ffn1_gu, w_ffn1_down, g_mix, w_mix_out, g_ffn2, w_ffn2_gu, w_ffn2_down, w_in_even, g_sgu, w_sgu, b_sgu, w_in_odd, g_q_c, g_k_c, g_q_d, g_k_d, sink_c, rpb_d):
    x = jnp.concatenate([x_prompt.reshape(N_PROMPT, D_MODEL), x_sample.reshape(N_SAMPLE, D_MODEL)], axis=0)
    cvec = jnp.concatenate([c_ctx[None, :], c, jnp.zeros((ADA_ROWS - N_GROUPS, D_MODEL), F32)], axis=0)
    mod_all = _ada_mod(cvec, w_ada, b_ada)[:, :N_GROUPS].reshape(DEPTH, N_GROUPS, N_MOD, D_MODEL)

    states = [[], [], [], []]
    for layer in range(DEPTH):
        mod = mod_all[layer]
        x = _ffn(x, g_ffn1[layer], mod, *_ffn_weights(w_ffn1_gu[layer], w_ffn1_down[layer]), first=0)
        if layer % 2 == 0:
            e = layer // 2
            proj = _inproj(x, g_mix[layer], mod, w_in_even[e].astype(BF16))
            o_p = _even_mixer(proj, g_sgu[e], w_sgu[e], b_sgu[e], seq=SEQ, n_seq=BATCH, row0=0, split=False)
            o_a, o_b = _even_mixer(proj, g_sgu[e], w_sgu[e], b_sgu[e], seq=DEC_SEQ, n_seq=DEC_BATCH, row0=N_PROMPT, split=True)
        else:
            o = layer // 2
            proj = _inproj(x, g_mix[layer], mod, w_in_odd[o].astype(BF16))
            o_p, ck, cv, dk, dv = _odd_context(proj, g_q_c[o], g_k_c[o], g_q_d[o], g_k_d[o], sink_c[o])
            for lst, s in zip(states, (ck, cv, dk, dv)):
                lst.append(s)
            o_a = _odd_latent_c(proj, cache_c_k, cache_c_v, o, g_q_c[o], g_k_c[o], sink_c[o])
            o_b = _odd_latent_d(proj, cache_d_k, cache_d_v, o, g_q_d[o], g_k_d[o], rpb_d[o])
        x = _outproj(x, mod, o_p, o_a, o_b, w_mix_out[layer].astype(BF16))
        x = _ffn(x, g_ffn2[layer], mod, *_ffn_weights(w_ffn2_gu[layer], w_ffn2_down[layer]), first=6)

    y_prompt = x[:N_PROMPT].reshape(BATCH, SEQ, D_MODEL)
    y_sample = x[N_PROMPT:].reshape(DEC_BATCH, DEC_SEQ, D_MODEL)
    return (y_prompt, y_sample) + tuple(jnp.stack(s, axis=1) for s in states)
```
